```python
import jax, jax.numpy as jnp
from jax import lax
import numpy as np

D_MODEL = 2048
BATCH = 8
SEQ = 2048
DEPTH = 2
DEC_BATCH = 1
DEC_SEQ = 8192
PAST_LEN = 128

GRID_W = 64
HEAD_DIM = 128
ATTN_WIDTH = D_MODEL // 2
N_Q_HEADS = ATTN_WIDTH // HEAD_DIM
N_KV_HEADS = 2
Q_PER_KV = N_Q_HEADS // N_KV_HEADS
KV_WIDTH = N_KV_HEADS * HEAD_DIM
POOL_WIDTH = D_MODEL - ATTN_WIDTH
POOL_WINDOWS = (2, 4, 8, 16)
N_POOL_GROUPS = len(POOL_WINDOWS)
POOL_GROUP_WIDTH = POOL_WIDTH // N_POOL_GROUPS
MIX_WIDTH = ATTN_WIDTH + POOL_WIDTH
IN_WIDTH = ATTN_WIDTH + 2 * KV_WIDTH + POOL_WIDTH
Q_BLOCK = 128
ROPE_THETA = 10000.0
N_EXPERT_GROUPS = 4
EXPERTS_PER_GROUP = 4
N_EXPERTS = N_EXPERT_GROUPS * EXPERTS_PER_GROUP
TOP_K_IN_GROUP = 2
D_EXPERT = D_MODEL // 4
EPS = 1e-6

kernel_name = "hymba_pool_axialgqa_hmoe_encoder"


def rms_norm(x, g):
    xf = x.astype(jnp.float32)
    y = xf * lax.rsqrt(jnp.mean(xf * xf, axis=-1, keepdims=True) + EPS)
    return (y * g.astype(jnp.float32)).astype(x.dtype)


def axial_angles(T):
    rows = T // GRID_W
    row = jnp.repeat(jnp.arange(rows, dtype=jnp.float32), GRID_W)
    col = jnp.tile(jnp.arange(GRID_W, dtype=jnp.float32), rows)
    axis_dim = HEAD_DIM // 2
    freqs = ROPE_THETA ** (-jnp.arange(0, axis_dim, 2, dtype=jnp.float32) / axis_dim)
    return row[:, None] * freqs, col[:, None] * freqs


def apply_axial_rope(x, ang_row, ang_col):
    T = x.shape[1]
    n_mid = x.ndim - 3

    def rot(xs, ang):
        a = ang.reshape((T,) + (1,) * n_mid + (ang.shape[-1],))
        c, s = jnp.cos(a), jnp.sin(a)
        x1, x2 = jnp.split(xs, 2, axis=-1)
        return jnp.concatenate([x1 * c - x2 * s, x2 * c + x1 * s], axis=-1)

    xf = x.astype(jnp.float32)
    half = HEAD_DIM // 2
    out = jnp.concatenate([rot(xf[..., :half], ang_row), rot(xf[..., half:], ang_col)], axis=-1)
    return out.astype(x.dtype)


def attention_group(q, k, v):
    B, T = q.shape[0], q.shape[1]
    n_blk = T // Q_BLOCK
    qb = q.reshape(B, n_blk, Q_BLOCK, N_KV_HEADS, Q_PER_KV, HEAD_DIM).transpose(1, 0, 2, 3, 4, 5)
    scale = HEAD_DIM ** -0.5

    def block(q_blk):
        s = jnp.einsum('bqkgd,bskd->bkgqs', q_blk, k).astype(jnp.float32) * scale
        p = jax.nn.softmax(s, axis=-1).astype(v.dtype)
        return jnp.einsum('bkgqs,bskd->bqkgd', p, v)

    o = lax.map(block, qb)
    return o.transpose(1, 0, 2, 3, 4, 5).reshape(B, T, ATTN_WIDTH)


def pool_group(u, pool_w, pool_scale):
    T = u.shape[1]
    uf = u.astype(jnp.float32)
    c = jnp.concatenate([jnp.zeros_like(uf[:, :1]), jnp.cumsum(uf, axis=1)], axis=1)
    t = jnp.arange(T)
    outs = []
    for gi, w in enumerate(POOL_WINDOWS):
        lo = jnp.clip(t - w // 2, 0, T)
        hi = jnp.clip(t + w - w // 2, 0, T)
        sl = slice(gi * POOL_GROUP_WIDTH, (gi + 1) * POOL_GROUP_WIDTH)
        cg = c[..., sl]
        mean = (jnp.take(cg, hi, axis=1) - jnp.take(cg, lo, axis=1)) / (hi - lo).astype(jnp.float32)[:, None]
        outs.append(mean - uf[..., sl])
    d = jnp.stack(outs, axis=2).astype(u.dtype)
    y = jnp.einsum('btgc,gcd->btgd', d, pool_w).reshape(u.shape)
    return y * pool_scale


def hierarchical_moe(x, router_group, router_expert, w_gate, w_up, w_down):
    B, T, D = x.shape
    xt = x.reshape(B * T, D)
    g_logits = (xt @ router_group).astype(jnp.float32)
    g_prob = jax.nn.softmax(g_logits, axis=-1)
    g_idx = jnp.argmax(g_logits, axis=-1)
    g_w = jnp.take_along_axis(g_prob, g_idx[:, None], axis=-1)
    e_logits = (xt @ router_expert).astype(jnp.float32).reshape(-1, N_EXPERT_GROUPS, EXPERTS_PER_GROUP)
    e_logits = jnp.take_along_axis(e_logits, g_idx[:, None, None], axis=1)[:, 0]
    top_v, top_i = lax.top_k(e_logits, TOP_K_IN_GROUP)
    top_w = jax.nn.softmax(top_v, axis=-1) * g_w
    e_idx = g_idx[:, None] * EXPERTS_PER_GROUP + top_i
    gate = jnp.sum(jax.nn.one_hot(e_idx, N_EXPERTS, dtype=jnp.float32) * top_w[..., None], axis=1)
    h = jax.nn.silu(jnp.einsum('nd,edf->nef', xt, w_gate)) * jnp.einsum('nd,edf->nef', xt, w_up)
    h = h * gate[..., None].astype(h.dtype)
    y = jnp.einsum('nef,efd->nd', h, w_down)
    return y.reshape(B, T, D)


def trunk(x, norm1_g, w_in, q_norm_g, k_norm_g, pool_w, pool_scale, w_out,
          norm2_g, router_group, router_expert, w_gate, w_up, w_down, final_g):
    B, T = x.shape[0], x.shape[1]
    ang_row, ang_col = axial_angles(T)
    for l in range(DEPTH):
        h = rms_norm(x, norm1_g[l])
        proj = h @ w_in[l]
        q, k, v, u = jnp.split(proj, [ATTN_WIDTH, ATTN_WIDTH + KV_WIDTH, ATTN_WIDTH + 2 * KV_WIDTH], axis=-1)
        q = q.reshape(B, T, N_KV_HEADS, Q_PER_KV, HEAD_DIM)
        k = k.reshape(B, T, N_KV_HEADS, HEAD_DIM)
        v = v.reshape(B, T, N_KV_HEADS, HEAD_DIM)
        q = apply_axial_rope(rms_norm(q, q_norm_g[l]), ang_row, ang_col)
        k = apply_axial_rope(rms_norm(k, k_norm_g[l]), ang_row, ang_col)
        a = attention_group(q, k, v)
        p = pool_group(u, pool_w[l], pool_scale[l])
        x = x + jnp.concatenate([a, p], axis=-1) @ w_out[l]
        x = x + hierarchical_moe(rms_norm(x, norm2_g[l]), router_group[l], router_expert[l],
                                 w_gate[l], w_up[l], w_down[l])
    return rms_norm(x, final_g)


def setup_inputs(seed: int = 0) -> dict:
    key = jax.random.key(seed)
    ks = jax.random.split(key, 16)
    f32 = jnp.float32
    nrm = lambda k, shape, scale: jax.random.normal(k, shape, f32) * scale
    gain = lambda k, shape: 1.0 + 0.02 * jax.random.normal(k, shape, f32)
    return {
        "x_prompt": jax.random.normal(ks[0], (BATCH, SEQ, D_MODEL), f32),
        "x_sample": jax.random.normal(ks[1], (DEC_BATCH, DEC_SEQ, D_MODEL), f32),
        "norm1_g": gain(ks[2], (DEPTH, D_MODEL)),
        "w_in": nrm(ks[3], (DEPTH, D_MODEL, IN_WIDTH), D_MODEL ** -0.5),
        "q_norm_g": gain(ks[4], (DEPTH, HEAD_DIM)),
        "k_norm_g": gain(ks[5], (DEPTH, HEAD_DIM)),
        "pool_w": nrm(ks[6], (DEPTH, N_POOL_GROUPS, POOL_GROUP_WIDTH, POOL_GROUP_WIDTH), POOL_GROUP_WIDTH ** -0.5),
        "pool_scale": gain(ks[7], (DEPTH, POOL_WIDTH)),
        "w_out": nrm(ks[8], (DEPTH, MIX_WIDTH, D_MODEL), MIX_WIDTH ** -0.5),
        "norm2_g": gain(ks[9], (DEPTH, D_MODEL)),
        "router_group": nrm(ks[10], (DEPTH, D_MODEL, N_EXPERT_GROUPS), D_MODEL ** -0.5),
        "router_expert": nrm(ks[11], (DEPTH, D_MODEL, N_EXPERTS), D_MODEL ** -0.5),
        "w_gate": nrm(ks[12], (DEPTH, N_EXPERTS, D_MODEL, D_EXPERT), D_MODEL ** -0.5),
        "w_up": nrm(ks[13], (DEPTH, N_EXPERTS, D_MODEL, D_EXPERT), D_MODEL ** -0.5),
        "w_down": nrm(ks[14], (DEPTH, N_EXPERTS, D_EXPERT, D_MODEL), D_EXPERT ** -0.5),
        "final_g": gain(ks[15], (D_MODEL,)),
    }


def reference(x_prompt, x_sample, norm1_g, w_in, q_norm_g, k_norm_g, pool_w, pool_scale, w_out,
              norm2_g, router_group, router_expert, w_gate, w_up, w_down, final_g):
    y_prompt = trunk(x_prompt, norm1_g, w_in, q_norm_g, k_norm_g, pool_w, pool_scale, w_out,
                     norm2_g, router_group, router_expert, w_gate, w_up, w_down, final_g)
    y_sample = trunk(x_sample, norm1_g, w_in, q_norm_g, k_norm_g, pool_w, pool_scale, w_out,
                     norm2_g, router_group, router_expert, w_gate, w_up, w_down, final_g)
    return (y_prompt, y_sample)
```

```python
import functools

import numpy as np
import jax
import jax.numpy as jnp
from jax import lax
from jax.experimental import pallas as pl
from jax.experimental.pallas import tpu as pltpu

F32 = jnp.float32
BF16 = jnp.bfloat16

GRID_W = 64
HEAD_DIM = 128
N_KV_HEADS = 2
Q_PER_KV = 4
POOL_WINDOWS = (2, 4, 8, 16)
ROPE_THETA = 10000.0
N_EXPERT_GROUPS = 4
EXPERTS_PER_GROUP = 4
PAIRS_PER_GROUP = 6
N_CLASSES = N_EXPERT_GROUPS * PAIRS_PER_GROUP
EPS = 1e-6

LANES = 128
SUBLANES = 8
VMEM_LIMIT_BYTES = 56 * 1024 * 1024

HALO = SUBLANES
ROUTER_LANES = LANES


def _tiles(n_rows):
    tm = min(512, n_rows)
    return tm


def _rms(x, g):
    ms = jnp.mean(x * x, axis=-1, keepdims=True)
    return x * lax.rsqrt(ms + EPS) * g


def _rows_from_3d(ref):
    return jnp.concatenate([ref[:, j, :] for j in range(ref.shape[1])], axis=-1)


def _rows_to_3d(ref, val):
    for j in range(ref.shape[1]):
        ref[:, j, :] = val[:, j * LANES:(j + 1) * LANES]


def _pick_rows(i, n_a_tiles, refs):
    if len(refs) == 2:
        return jnp.where(i < n_a_tiles, refs[0][...], refs[1][...])
    return _rows_from_3d(refs[0])


def _in_proj_kernel(*refs, n_x, n_a_tiles, attn_w, kv_w):
    x_refs = refs[:n_x]
    (g1_ref, w_ref, qg_ref, kg_ref, cos_ref, sin_ref,
     q_ref, kt_ref, v_ref, u_ref) = refs[n_x:]
    i = pl.program_id(0)
    x = _pick_rows(i, n_a_tiles, x_refs)
    h = _rms(x, g1_ref[...]).astype(BF16)
    proj = jnp.dot(h, w_ref[...], preferred_element_type=F32)

    cos = cos_ref[...]
    sin = sin_ref[...]
    lane = lax.broadcasted_iota(jnp.int32, cos.shape, 1)
    low_half = (lane & (HEAD_DIM // 4)) == 0

    def norm_rope(xh, g):
        y = _rms(xh, g)
        partner = jnp.where(low_half, pltpu.roll(y, HEAD_DIM - HEAD_DIM // 4, 1),
                            pltpu.roll(y, HEAD_DIM // 4, 1))
        return y * cos + partner * sin

    qg = qg_ref[...]
    for hq in range(attn_w // HEAD_DIM):
        sl = slice(hq * HEAD_DIM, (hq + 1) * HEAD_DIM)
        q_ref[:, sl] = norm_rope(proj[:, sl], qg).astype(BF16)
    kg = kg_ref[...]
    for hk in range(N_KV_HEADS):
        sl = slice(attn_w + hk * HEAD_DIM, attn_w + (hk + 1) * HEAD_DIM)
        kt_ref[hk * HEAD_DIM:(hk + 1) * HEAD_DIM, :] = norm_rope(proj[:, sl], kg).T.astype(BF16)
    v_ref[...] = proj[:, attn_w + kv_w:attn_w + 2 * kv_w].astype(BF16)
    u_ref[...] = proj[:, attn_w + 2 * kv_w:]


def _in_proj(xs, seqs, g1, w_in, qg, kg, cos_t, sin_t):
    n = sum(b * t for b, t in seqs)
    d, in_w = w_in.shape
    attn_w = d // 2
    kv_w = N_KV_HEADS * HEAD_DIM
    pool_w = in_w - attn_w - 2 * kv_w
    tm = _tiles(min(t for _, t in seqs))
    n_tiles = n // tm
    n_a_tiles = seqs[0][0] * seqs[0][1] // tm
    tps = [t // tm for _, t in seqs]

    def pos_map(i):
        return (jnp.where(i < n_a_tiles, i % tps[0], (i - n_a_tiles) % tps[1]), 0)

    if len(xs) == 2:
        x_specs = [pl.BlockSpec((tm, d), lambda i: (jnp.minimum(i, n_a_tiles - 1), 0)),
                   pl.BlockSpec((tm, d), lambda i: (jnp.maximum(i - n_a_tiles, 0), 0))]
    else:
        x_specs = [pl.BlockSpec((tm, d // LANES, LANES), lambda i: (i, 0, 0))]
    const = lambda i: (0, 0)
    return pl.pallas_call(
        functools.partial(_in_proj_kernel, n_x=len(xs), n_a_tiles=n_a_tiles,
                          attn_w=attn_w, kv_w=kv_w),
        grid=(n_tiles,),
        in_specs=x_specs + [
            pl.BlockSpec((1, d), const),
            pl.BlockSpec((d, in_w), const, pipeline_mode=pl.Buffered(1)),
            pl.BlockSpec((1, HEAD_DIM), const),
            pl.BlockSpec((1, HEAD_DIM), const),
            pl.BlockSpec((tm, HEAD_DIM), pos_map),
            pl.BlockSpec((tm, HEAD_DIM), pos_map),
        ],
        out_specs=[
            pl.BlockSpec((tm, attn_w), lambda i: (i, 0)),
            pl.BlockSpec((kv_w, tm), lambda i: (0, i)),
            pl.BlockSpec((tm, kv_w), lambda i: (i, 0)),
            pl.BlockSpec((tm, pool_w), lambda i: (i, 0)),
        ],
        out_shape=[
            jax.ShapeDtypeStruct((n, attn_w), BF16),
            jax.ShapeDtypeStruct((kv_w, n), BF16),
            jax.ShapeDtypeStruct((n, kv_w), BF16),
            jax.ShapeDtypeStruct((n, pool_w), F32),
        ],
        compiler_params=pltpu.CompilerParams(
            dimension_semantics=("arbitrary",), vmem_limit_bytes=VMEM_LIMIT_BYTES),
        name="in_proj",
    )(*xs, g1, w_in, qg, kg, cos_t, sin_t)


def _attn_kernel(qt_ref, kg_ref, kb_ref, first_ref, last_ref,
                 q_ref, kt_ref, v_ref, o_ref, m_ref, l_ref, acc_ref, *, tkc, scale):
    s_idx = pl.program_id(0)
    n_chunks = kt_ref.shape[1] // tkc

    @pl.when(first_ref[s_idx] == 1)
    def _():
        m_ref[...] = jnp.full(m_ref.shape, -jnp.inf, F32)
        l_ref[...] = jnp.zeros(l_ref.shape, F32)
        acc_ref[...] = jnp.zeros(acc_ref.shape, F32)

    for hq in range(Q_PER_KV):
        q = q_ref[:, hq * HEAD_DIM:(hq + 1) * HEAD_DIM]
        m_prev = m_ref[hq]
        l_prev = l_ref[hq]
        acc = acc_ref[hq]
        for c in range(n_chunks):
            s = jnp.dot(q, kt_ref[:, c * tkc:(c + 1) * tkc], preferred_element_type=F32)
            m_new = jnp.maximum(m_prev, jnp.max(s, axis=-1, keepdims=True))
            p = jnp.exp((s - m_new) * scale)
            alpha = jnp.exp((m_prev - m_new) * scale)
            l_prev = alpha * l_prev + jnp.sum(p, axis=-1, keepdims=True)
            acc = acc * alpha + jnp.dot(p.astype(BF16), v_ref[c * tkc:(c + 1) * tkc, :],
                                        preferred_element_type=F32)
            m_prev = m_new
        m_ref[hq] = m_prev
        l_ref[hq] = l_prev
        acc_ref[hq] = acc

    @pl.when(last_ref[s_idx] == 1)
    def _():
        for hq in range(Q_PER_KV):
            o_ref[:, hq * HEAD_DIM:(hq + 1) * HEAD_DIM] = (acc_ref[hq] / l_ref[hq]).astype(BF16)


def _attention(q, kt, v, seqs):
    n, attn_w = q.shape
    t_min = min(t for _, t in seqs)
    tq = min(512, t_min)
    tkb = min(2048, t_min)
    tkc = min(512, tkb)
    qt, kg, kb, first, last = [], [], [], [], []
    row0 = 0
    for b, t in seqs:
        for _ in range(b):
            for g in range(N_KV_HEADS):
                for iq in range(t // tq):
                    for ik in range(t // tkb):
                        qt.append(row0 // tq + iq)
                        kg.append(g)
                        kb.append(row0 // tkb + ik)
                        first.append(int(ik == 0))
                        last.append(int(ik == t // tkb - 1))
            row0 += t
    tabs = [jnp.asarray(np.asarray(a, np.int32)) for a in (qt, kg, kb, first, last)]
    gw = Q_PER_KV * HEAD_DIM
    grid_spec = pltpu.PrefetchScalarGridSpec(
        num_scalar_prefetch=5,
        grid=(len(qt),),
        in_specs=[
            pl.BlockSpec((tq, gw), lambda s, qt, kg, kb, f, l: (qt[s], kg[s])),
            pl.BlockSpec((HEAD_DIM, tkb), lambda s, qt, kg, kb, f, l: (kg[s], kb[s])),
            pl.BlockSpec((tkb, HEAD_DIM), lambda s, qt, kg, kb, f, l: (kb[s], kg[s])),
        ],
        out_specs=pl.BlockSpec((tq, gw), lambda s, qt, kg, kb, f, l: (qt[s], kg[s])),
        scratch_shapes=[
            pltpu.VMEM((Q_PER_KV, tq, 1), F32),
            pltpu.VMEM((Q_PER_KV, tq, 1), F32),
            pltpu.VMEM((Q_PER_KV, tq, HEAD_DIM), F32),
        ],
    )
    return pl.pallas_call(
        functools.partial(_attn_kernel, tkc=tkc, scale=HEAD_DIM ** -0.5),
        grid_spec=grid_spec,
        out_shape=jax.ShapeDtypeStruct((n, attn_w), BF16),
        compiler_params=pltpu.CompilerParams(
            dimension_semantics=("arbitrary",), vmem_limit_bytes=VMEM_LIMIT_BYTES),
        name="attention",
    )(*tabs, q, kt, v)


def _mix_out_kernel(*refs, n_x, n_a_tiles, tps, seq_len):
    x_refs = refs[:n_x]
    (a_ref, u_ref, up_ref, un_ref, pw_ref, ps_ref, wo_ref, g2_ref, rw_ref,
     x1_ref, cls_ref, ext_ref) = refs[n_x:]
    i = pl.program_id(0)
    tm = u_ref.shape[0]
    is_a = i < n_a_tiles
    tile_in_seq = jnp.where(is_a, i % tps[0], (i - n_a_tiles) % tps[1])
    t_len = jnp.where(is_a, seq_len[0], seq_len[1])
    t0 = tile_in_seq * tm

    ext_ref[0:HALO, :] = jnp.where(t0 > 0, up_ref[...], 0.0)
    ext_ref[HALO:HALO + tm, :] = u_ref[...]
    ext_ref[HALO + tm:, :] = jnp.where(t0 + tm < t_len, un_ref[...], 0.0)

    t = t0 + lax.broadcasted_iota(jnp.int32, (tm, 1), 0)
    gw = u_ref.shape[1] // len(POOL_WINDOWS)
    pooled = []
    for gi, w in enumerate(POOL_WINDOWS):
        sl = slice(gi * gw, (gi + 1) * gw)
        tot = ext_ref[HALO - w // 2:HALO - w // 2 + tm, sl]
        for o in range(-w // 2 + 1, w - w // 2):
            tot = tot + ext_ref[HALO + o:HALO + o + tm, sl]
        lo = jnp.clip(t - w // 2, 0, t_len)
        hi = jnp.clip(t + w - w // 2, 0, t_len)
        mean = tot / (hi - lo).astype(F32)
        dgrp = (mean - u_ref[:, sl]).astype(BF16)
        y = jnp.dot(dgrp, pw_ref[gi], preferred_element_type=F32)
        pooled.append((y * ps_ref[:, sl]).astype(BF16))
    mix = jnp.concatenate([a_ref[...]] + pooled, axis=-1)
    x = _pick_rows(i, n_a_tiles, x_refs)
    x1 = x + jnp.dot(mix, wo_ref[...], preferred_element_type=F32)
    _rows_to_3d(x1_ref, x1)

    h2 = _rms(x1, g2_ref[...]).astype(BF16)
    logits = jnp.dot(h2, rw_ref[...], preferred_element_type=F32)
    lane = lax.broadcasted_iota(jnp.int32, logits.shape, 1)
    neg = jnp.float32(-jnp.inf)

    def first_max(vals):
        mx = jnp.max(vals, axis=-1, keepdims=True)
        idx = jnp.min(jnp.where(vals == mx, lane, ROUTER_LANES), axis=-1, keepdims=True)
        return idx

    g_idx = first_max(jnp.where(lane < N_EXPERT_GROUPS, logits, neg))
    e_base = N_EXPERT_GROUPS + g_idx * EXPERTS_PER_GROUP
    e_log = jnp.where((lane >= e_base) & (lane < e_base + EXPERTS_PER_GROUP), logits, neg)
    i1 = first_max(e_log)
    i2 = first_max(jnp.where(lane == i1, neg, e_log))
    e_lo = jnp.minimum(i1, i2) - e_base
    e_hi = jnp.maximum(i1, i2) - e_base
    pair = ((e_lo * (7 - e_lo)) >> 1) + e_hi - e_lo - 1
    cls_ref[...] = jnp.broadcast_to(g_idx * PAIRS_PER_GROUP + pair, cls_ref.shape)


def _mix_out(xs, seqs, a, u, pool_w, pool_scale, w_out, g2, router_w):
    n, pw = u.shape
    d = w_out.shape[1]
    tm = _tiles(min(t for _, t in seqs))
    n_tiles = n // tm
    n_a_tiles = seqs[0][0] * seqs[0][1] // tm
    tps = tuple(t // tm for _, t in seqs)
    hb = tm // HALO
    n_hb = n // HALO
    if len(xs) == 2:
        x_specs = [pl.BlockSpec((tm, d), lambda i: (jnp.minimum(i, n_a_tiles - 1), 0)),
                   pl.BlockSpec((tm, d), lambda i: (jnp.maximum(i - n_a_tiles, 0), 0))]
    else:
        x_specs = [pl.BlockSpec((tm, d // LANES, LANES), lambda i: (i, 0, 0))]
    const = lambda i: (0, 0)
    return pl.pallas_call(
        functools.partial(_mix_out_kernel, n_x=len(xs), n_a_tiles=n_a_tiles, tps=tps,
                          seq_len=tuple(t for _, t in seqs)),
        grid=(n_tiles,),
        in_specs=x_specs + [
            pl.BlockSpec((tm, a.shape[1]), lambda i: (i, 0)),
            pl.BlockSpec((tm, pw), lambda i: (i, 0)),
            pl.BlockSpec((HALO, pw), lambda i: (jnp.maximum(i * hb - 1, 0), 0)),
            pl.BlockSpec((HALO, pw), lambda i: (jnp.minimum((i + 1) * hb, n_hb - 1), 0)),
            pl.BlockSpec(pool_w.shape, lambda i: (0, 0, 0), pipeline_mode=pl.Buffered(1)),
            pl.BlockSpec((1, pw), const),
            pl.BlockSpec(w_out.shape, const, pipeline_mode=pl.Buffered(1)),
            pl.BlockSpec((1, d), const),
            pl.BlockSpec(router_w.shape, const, pipeline_mode=pl.Buffered(1)),
        ],
        out_specs=[
            pl.BlockSpec((tm, d // LANES, LANES), lambda i: (i, 0, 0)),
            pl.BlockSpec((tm, LANES), lambda i: (i, 0)),
        ],
        out_shape=[
            jax.ShapeDtypeStruct((n, d // LANES, LANES), F32),
            jax.ShapeDtypeStruct((n, LANES), jnp.int32),
        ],
        scratch_shapes=[pltpu.VMEM((tm + 2 * HALO, pw), F32)],
        compiler_params=pltpu.CompilerParams(
            dimension_semantics=("arbitrary",), vmem_limit_bytes=VMEM_LIMIT_BYTES),
        name="mix_out",
    )(*xs, a, u, u, u, pool_w, pool_scale, w_out, g2, router_w)


def _moe_kernel(src_ref, nvalid_ref, grp_ref, elo_ref, ehi_ref,
                x1_hbm, g2_ref, rw_ref, wg_lo, wu_lo, wd_lo, wg_hi, wu_hi, wd_hi,
                x2_hbm, gbuf, obuf, gsem, ssem):
    t = pl.program_id(0)
    n_tiles = pl.num_programs(0)
    tm = obuf.shape[0]

    def row_copy(tile, r, slot):
        return pltpu.make_async_copy(x1_hbm.at[src_ref[tile * tm + r]], gbuf.at[slot, r],
                                     gsem.at[slot])

    def start_gather(tile, slot):
        def body(r, c):
            row_copy(tile, r, slot).start()
            return c
        lax.fori_loop(0, nvalid_ref[tile], body, 0)

    @pl.when(t == 0)
    def _():
        gbuf[...] = jnp.zeros(gbuf.shape, F32)
        start_gather(0, 0)

    @pl.when(t + 1 < n_tiles)
    def _():
        start_gather(t + 1, (t + 1) % 2)

    slot = t % 2
    nv = nvalid_ref[t]

    def wait_body(r, c):
        row_copy(t, r, slot).wait()
        return c
    lax.fori_loop(0, nv, wait_body, 0)

    @pl.when(nv > 0)
    def _():
        x = _rows_from_3d(gbuf.at[slot])
        h2 = _rms(x, g2_ref[...]).astype(BF16)
        logits = jnp.dot(h2, rw_ref[...], preferred_element_type=F32)
        lane = lax.broadcasted_iota(jnp.int32, logits.shape, 1)

        def pick(idx):
            return jnp.sum(jnp.where(lane == idx, logits, 0.0), axis=-1, keepdims=True)

        is_grp = lane < N_EXPERT_GROUPS
        g_max = jnp.max(jnp.where(is_grp, logits, -jnp.inf), axis=-1, keepdims=True)
        g_den = jnp.sum(jnp.where(is_grp, jnp.exp(logits - g_max), 0.0), axis=-1, keepdims=True)
        grp = grp_ref[t]
        g_w = jnp.exp(pick(grp) - g_max) / g_den
        v_lo = pick(N_EXPERT_GROUPS + elo_ref[t])
        v_hi = pick(N_EXPERT_GROUPS + ehi_ref[t])
        v_max = jnp.maximum(v_lo, v_hi)
        p_lo = jnp.exp(v_lo - v_max)
        p_hi = jnp.exp(v_hi - v_max)
        w_lo = p_lo / (p_lo + p_hi) * g_w
        w_hi = p_hi / (p_lo + p_hi) * g_w

        def expert(wg, wu, wd, gate):
            g = jnp.dot(h2, wg[...], preferred_element_type=F32)
            u = jnp.dot(h2, wu[...], preferred_element_type=F32)
            hh = (g * (1.0 / (1.0 + jnp.exp(-g)))) * u * gate
            return jnp.dot(hh.astype(BF16), wd[...], preferred_element_type=F32)

        y = expert(wg_lo, wu_lo, wd_lo, w_lo) + expert(wg_hi, wu_hi, wd_hi, w_hi)
        _rows_to_3d(obuf, x + y)

        def out_copy(r):
            return pltpu.make_async_copy(obuf.at[r], x2_hbm.at[src_ref[t * tm + r]], ssem)

        def s_start(r, c):
            out_copy(r).start()
            return c
        lax.fori_loop(0, nv, s_start, 0)

        def s_wait(r, c):
            out_copy(r).wait()
            return c
        lax.fori_loop(0, nv, s_wait, 0)


def _moe(x1, cls, g2, router_w, w_gate, w_up, w_down, tm):
    n, s, _ = x1.shape
    d = s * LANES
    n_exp, _, d_e = w_gate.shape
    n_tiles = n // tm + N_CLASSES

    order = jnp.argsort(cls, stable=True).astype(jnp.int32)
    cnt = jnp.sum(cls[:, None] == jnp.arange(N_CLASSES, dtype=jnp.int32)[None, :], axis=0,
                  dtype=jnp.int32)
    tiles_c = (cnt + tm - 1) // tm
    tile_end = jnp.cumsum(tiles_c)
    tile_start = tile_end - tiles_c
    tok_start = jnp.cumsum(cnt) - cnt
    tile_id = jnp.arange(n_tiles, dtype=jnp.int32)
    used = tile_id < tile_end[-1]
    tile_cls = jnp.searchsorted(tile_end, jnp.minimum(tile_id, tile_end[-1] - 1),
                                side="right").astype(jnp.int32)
    tile_cls = jnp.minimum(tile_cls, N_CLASSES - 1)
    j0 = (tile_id - tile_start[tile_cls]) * tm
    nvalid = jnp.where(used, jnp.clip(cnt[tile_cls] - j0, 0, tm), 0).astype(jnp.int32)
    slot_j = j0[:, None] + jnp.arange(tm, dtype=jnp.int32)[None, :]
    slot_tok = tok_start[tile_cls][:, None] + jnp.minimum(slot_j, cnt[tile_cls][:, None] - 1)
    src = order[jnp.clip(slot_tok, 0, n - 1)].reshape(-1).astype(jnp.int32)

    grp = tile_cls // PAIRS_PER_GROUP
    pair = tile_cls % PAIRS_PER_GROUP
    pair_lo = jnp.asarray(np.asarray([0, 0, 0, 1, 1, 2], np.int32))
    pair_hi = jnp.asarray(np.asarray([1, 2, 3, 2, 3, 3], np.int32))
    e_lo = grp * EXPERTS_PER_GROUP + pair_lo[pair]
    e_hi = grp * EXPERTS_PER_GROUP + pair_hi[pair]

    const = lambda t, *_: (0, 0)
    w_in_spec = lambda sel: pl.BlockSpec(
        (None, d, d_e), lambda t, src, nv, grp, elo, ehi: ((elo, ehi)[sel][t], 0, 0))
    w_dn_spec = lambda sel: pl.BlockSpec(
        (None, d_e, d), lambda t, src, nv, grp, elo, ehi: ((elo, ehi)[sel][t], 0, 0))
    grid_spec = pltpu.PrefetchScalarGridSpec(
        num_scalar_prefetch=5,
        grid=(n_tiles,),
        in_specs=[
            pl.BlockSpec(memory_space=pl.ANY),
            pl.BlockSpec((1, d), const),
            pl.BlockSpec(router_w.shape, const),
            w_in_spec(0), w_in_spec(0), w_dn_spec(0),
            w_in_spec(1), w_in_spec(1), w_dn_spec(1),
        ],
        out_specs=pl.BlockSpec(memory_space=pl.ANY),
        scratch_shapes=[
            pltpu.VMEM((2, tm, s, LANES), F32),
            pltpu.VMEM((tm, s, LANES), F32),
            pltpu.SemaphoreType.DMA((2,)),
            pltpu.SemaphoreType.DMA,
        ],
    )
    return pl.pallas_call(
        _moe_kernel,
        grid_spec=grid_spec,
        out_shape=jax.ShapeDtypeStruct((n, s, LANES), F32),
        compiler_params=pltpu.CompilerParams(
            dimension_semantics=("arbitrary",), vmem_limit_bytes=VMEM_LIMIT_BYTES),
        name="moe",
    )(src, nvalid, grp.astype(jnp.int32), e_lo.astype(jnp.int32), e_hi.astype(jnp.int32),
      x1, g2, router_w, w_gate, w_up, w_down, w_gate, w_up, w_down)


def _final_kernel(x_ref, g_ref, o_ref):
    o_ref[...] = _rms(_rows_from_3d(x_ref), g_ref[...])


def _final_norm(x2, g, row0, rows):
    _, s, _ = x2.shape
    d = s * LANES
    tm = _tiles(rows)
    return pl.pallas_call(
        _final_kernel,
        grid=(rows // tm,),
        in_specs=[pl.BlockSpec((tm, s, LANES), lambda i: (i + row0 // tm, 0, 0)),
                  pl.BlockSpec((1, d), lambda i: (0, 0))],
        out_specs=pl.BlockSpec((tm, d), lambda i: (i, 0)),
        out_shape=jax.ShapeDtypeStruct((rows, d), F32),
        compiler_params=pltpu.CompilerParams(
            dimension_semantics=("arbitrary",), vmem_limit_bytes=VMEM_LIMIT_BYTES),
        name="final_norm",
    )(x2, g)


def _rope_tables(t_max):
    rows = t_max // GRID_W
    row = jnp.repeat(jnp.arange(rows, dtype=F32), GRID_W)
    col = jnp.tile(jnp.arange(GRID_W, dtype=F32), rows)
    axis_dim = HEAD_DIM // 2
    freqs = ROPE_THETA ** (-jnp.arange(0, axis_dim, 2, dtype=F32) / axis_dim)
    ar, ac = row[:, None] * freqs, col[:, None] * freqs
    cos_t = jnp.concatenate([jnp.cos(ar), jnp.cos(ar), jnp.cos(ac), jnp.cos(ac)], axis=-1)
    sin_t = jnp.concatenate([-jnp.sin(ar), jnp.sin(ar), -jnp.sin(ac), jnp.sin(ac)], axis=-1)
    return cos_t, sin_t


def kernel(x_prompt, x_sample, norm1_g, w_in, q_norm_g, k_norm_g, pool_w, pool_scale, w_out,
           norm2_g, router_group, router_expert, w_gate, w_up, w_down, final_g):
    depth, d, _ = w_in.shape
    seqs = (x_prompt.shape[:2], x_sample.shape[:2])
    n_a = seqs[0][0] * seqs[0][1]
    n_b = seqs[1][0] * seqs[1][1]
    cos_t, sin_t = _rope_tables(max(t for _, t in seqs))
    router_w = jnp.concatenate([router_group, router_expert], axis=-1)
    router_w = jnp.pad(router_w, ((0, 0), (0, 0), (0, ROUTER_LANES - router_w.shape[-1])))
    router_w = router_w.astype(BF16)
    w_in_b, w_out_b, pool_w_b = w_in.astype(BF16), w_out.astype(BF16), pool_w.astype(BF16)
    w_gate_b, w_up_b, w_down_b = w_gate.astype(BF16), w_up.astype(BF16), w_down.astype(BF16)
    moe_tm = min(256, n_a, n_b)

    xs = (x_prompt.reshape(n_a, d), x_sample.reshape(n_b, d))
    for l in range(depth):
        q, kt, v, u = _in_proj(xs, seqs, norm1_g[l][None], w_in_b[l], q_norm_g[l][None],
                               k_norm_g[l][None], cos_t, sin_t)
        a = _attention(q, kt, v, seqs)
        x1, cls = _mix_out(xs, seqs, a, u, pool_w_b[l], pool_scale[l][None], w_out_b[l],
                           norm2_g[l][None], router_w[l])
        x2 = _moe(x1, cls[:, 0], norm2_g[l][None], router_w[l], w_gate_b[l], w_up_b[l],
                  w_down_b[l], moe_tm)
        xs = (x2,)
    y_a = _final_norm(xs[0], final_g[None], 0, n_a).reshape(x_prompt.shape)
    y_b = _final_norm(xs[0], final_g[None], n_a, n_b).reshape(x_sample.shape)
    return (y_a, y_b)
```

```python
import functools

import numpy as np
import jax
import jax.numpy as jnp
from jax import lax
from jax.experimental import pallas as pl
from jax.experimental.pallas import tpu as pltpu

F32 = jnp.float32
BF16 = jnp.bfloat16

GRID_W = 64
HEAD_DIM = 128
N_KV_HEADS = 2
Q_PER_KV = 4
POOL_WINDOWS = (2, 4, 8, 16)
ROPE_THETA = 10000.0
N_EXPERT_GROUPS = 4
EXPERTS_PER_GROUP = 4
PAIRS_PER_GROUP = 6
N_CLASSES = N_EXPERT_GROUPS * PAIRS_PER_GROUP
EPS = 1e-6

LANES = 128
SUBLANES = 8
VMEM_LIMIT_BYTES = 56 * 1024 * 1024

HALO = SUBLANES
ROUTER_LANES = LANES
SOFTMAX_ROWS = 32
LOG2_E = 1.4426950408889634


def _tiles(n_rows):
    tm = min(512, n_rows)
    return tm


def _rms(x, g):
    ms = jnp.mean(x * x, axis=-1, keepdims=True)
    return x * lax.rsqrt(ms + EPS) * g


def _pick_rows(i, n_a_tiles, refs):
    if len(refs) == 2:
        return jnp.where(i < n_a_tiles, refs[0][...], refs[1][...])
    return refs[0][...]


def _in_proj_kernel(*refs, n_x, n_a_tiles, attn_w, kv_w):
    x_refs = refs[:n_x]
    (g1_ref, w_ref, qg_ref, kg_ref, cos_ref, sin_ref,
     q_ref, kt_ref, v_ref, u_ref) = refs[n_x:]
    i = pl.program_id(0)
    x = _pick_rows(i, n_a_tiles, x_refs)
    h = _rms(x, g1_ref[...]).astype(BF16)
    proj = jnp.dot(h, w_ref[...], preferred_element_type=F32)

    cos = cos_ref[...]
    sin = sin_ref[...]
    lane = lax.broadcasted_iota(jnp.int32, cos.shape, 1)
    low_half = (lane & (HEAD_DIM // 4)) == 0

    def norm_rope(xh, g):
        y = _rms(xh, g)
        partner = jnp.where(low_half, pltpu.roll(y, HEAD_DIM - HEAD_DIM // 4, 1),
                            pltpu.roll(y, HEAD_DIM // 4, 1))
        return y * cos + partner * sin

    qg = qg_ref[...]
    for hq in range(attn_w // HEAD_DIM):
        sl = slice(hq * HEAD_DIM, (hq + 1) * HEAD_DIM)
        q_ref[:, sl] = norm_rope(proj[:, sl], qg).astype(BF16)
    kg = kg_ref[...]
    for hk in range(N_KV_HEADS):
        sl = slice(attn_w + hk * HEAD_DIM, attn_w + (hk + 1) * HEAD_DIM)
        kt_ref[hk * HEAD_DIM:(hk + 1) * HEAD_DIM, :] = norm_rope(proj[:, sl], kg).T.astype(BF16)
    v_ref[...] = proj[:, attn_w + kv_w:attn_w + 2 * kv_w].astype(BF16)
    u_ref[...] = proj[:, attn_w + 2 * kv_w:]


def _in_proj(xs, seqs, g1, w_in, qg, kg, cos_t, sin_t):
    n = sum(b * t for b, t in seqs)
    d, in_w = w_in.shape
    attn_w = d // 2
    kv_w = N_KV_HEADS * HEAD_DIM
    pool_w = in_w - attn_w - 2 * kv_w
    tm = _tiles(min(t for _, t in seqs))
    n_tiles = n // tm
    n_a_tiles = seqs[0][0] * seqs[0][1] // tm
    tps = [t // tm for _, t in seqs]

    def pos_map(i):
        return (jnp.where(i < n_a_tiles, i % tps[0], (i - n_a_tiles) % tps[1]), 0)

    if len(xs) == 2:
        x_specs = [pl.BlockSpec((tm, d), lambda i: (jnp.minimum(i, n_a_tiles - 1), 0)),
                   pl.BlockSpec((tm, d), lambda i: (jnp.maximum(i - n_a_tiles, 0), 0))]
    else:
        x_specs = [pl.BlockSpec((tm, d), lambda i: (i, 0))]
    const = lambda i: (0, 0)
    return pl.pallas_call(
        functools.partial(_in_proj_kernel, n_x=len(xs), n_a_tiles=n_a_tiles,
                          attn_w=attn_w, kv_w=kv_w),
        grid=(n_tiles,),
        in_specs=x_specs + [
            pl.BlockSpec((1, d), const),
            pl.BlockSpec((d, in_w), const, pipeline_mode=pl.Buffered(1)),
            pl.BlockSpec((1, HEAD_DIM), const),
            pl.BlockSpec((1, HEAD_DIM), const),
            pl.BlockSpec((tm, HEAD_DIM), pos_map),
            pl.BlockSpec((tm, HEAD_DIM), pos_map),
        ],
        out_specs=[
            pl.BlockSpec((tm, attn_w), lambda i: (i, 0)),
            pl.BlockSpec((kv_w, tm), lambda i: (0, i)),
            pl.BlockSpec((tm, kv_w), lambda i: (i, 0)),
            pl.BlockSpec((tm, pool_w), lambda i: (i, 0)),
        ],
        out_shape=[
            jax.ShapeDtypeStruct((n, attn_w), BF16),
            jax.ShapeDtypeStruct((kv_w, n), BF16),
            jax.ShapeDtypeStruct((n, kv_w), BF16),
            jax.ShapeDtypeStruct((n, pool_w), F32),
        ],
        compiler_params=pltpu.CompilerParams(
            dimension_semantics=("arbitrary",), vmem_limit_bytes=VMEM_LIMIT_BYTES),
        name="in_proj",
    )(*xs, g1, w_in, qg, kg, cos_t, sin_t)


def _attn_kernel(qt_ref, kg_ref, kb_ref, first_ref, last_ref,
                 q_ref, kt_ref, v_ref, o_ref, m_ref, l_ref, acc_ref, s_ref, p_ref, al_ref,
                 *, tkc, exp2_scale):
    s_idx = pl.program_id(0)
    tq = q_ref.shape[0]
    n_chunks = kt_ref.shape[1] // tkc
    iters = [(c, hq) for c in range(n_chunks) for hq in range(Q_PER_KV)]

    @pl.when(first_ref[s_idx] == 1)
    def _():
        m_ref[...] = jnp.full(m_ref.shape, -jnp.inf, F32)
        l_ref[...] = jnp.zeros(l_ref.shape, F32)
        acc_ref[...] = jnp.zeros(acc_ref.shape, F32)

    def scores(k):
        c, hq = iters[k]
        s_ref[k % 2] = jnp.dot(q_ref[:, hq * HEAD_DIM:(hq + 1) * HEAD_DIM],
                               kt_ref[:, c * tkc:(c + 1) * tkc], preferred_element_type=F32)

    def softmax(k):
        _, hq = iters[k]
        slot = k % 2
        for r0 in range(0, tq, SOFTMAX_ROWS):
            rows = slice(r0, r0 + SOFTMAX_ROWS)
            m_prev = m_ref[hq, rows, :]
            m_new = jnp.maximum(m_prev, jnp.max(s_ref[slot, rows, :], axis=-1, keepdims=True))
            al_ref[slot, rows, :] = jnp.exp2((m_prev - m_new) * exp2_scale)
            m_ref[hq, rows, :] = m_new
        for r0 in range(0, tq, SOFTMAX_ROWS):
            rows = slice(r0, r0 + SOFTMAX_ROWS)
            m_new = m_ref[hq, rows, :]
            p_sum = al_ref[slot, rows, :] * l_ref[hq, rows, :]
            for j in range(tkc // LANES):
                cols = slice(j * LANES, (j + 1) * LANES)
                p = jnp.exp2((s_ref[slot, rows, cols] - m_new) * exp2_scale)
                p_ref[slot, rows, cols] = p.astype(BF16)
                p_sum = p_sum + p
            l_ref[hq, rows, :] = p_sum

    def weighted_values(k):
        c, hq = iters[k]
        acc_ref[hq] = acc_ref[hq] * al_ref[k % 2] + jnp.dot(
            p_ref[k % 2], v_ref[c * tkc:(c + 1) * tkc, :], preferred_element_type=F32)

    scores(0)
    for k in range(len(iters)):
        if k + 1 < len(iters):
            scores(k + 1)
        if k >= 1:
            weighted_values(k - 1)
        softmax(k)
    weighted_values(len(iters) - 1)

    @pl.when(last_ref[s_idx] == 1)
    def _():
        for hq in range(Q_PER_KV):
            l_row = jnp.sum(l_ref[hq], axis=-1, keepdims=True)
            o_ref[:, hq * HEAD_DIM:(hq + 1) * HEAD_DIM] = (acc_ref[hq] / l_row).astype(BF16)


def _attention(q, kt, v, seqs):
    n, attn_w = q.shape
    t_min = min(t for _, t in seqs)
    tq = min(512, t_min)
    tkb = min(2048, t_min)
    tkc = min(512, tkb)
    qt, kg, kb, first, last = [], [], [], [], []
    row0 = 0
    for b, t in seqs:
        for _ in range(b):
            for g in range(N_KV_HEADS):
                for iq in range(t // tq):
                    for ik in range(t // tkb):
                        qt.append(row0 // tq + iq)
                        kg.append(g)
                        kb.append(row0 // tkb + ik)
                        first.append(int(ik == 0))
                        last.append(int(ik == t // tkb - 1))
            row0 += t
    tabs = [jnp.asarray(np.asarray(a, np.int32)) for a in (qt, kg, kb, first, last)]
    gw = Q_PER_KV * HEAD_DIM
    grid_spec = pltpu.PrefetchScalarGridSpec(
        num_scalar_prefetch=5,
        grid=(len(qt),),
        in_specs=[
            pl.BlockSpec((tq, gw), lambda s, qt, kg, kb, f, l: (qt[s], kg[s])),
            pl.BlockSpec((HEAD_DIM, tkb), lambda s, qt, kg, kb, f, l: (kg[s], kb[s])),
            pl.BlockSpec((tkb, HEAD_DIM), lambda s, qt, kg, kb, f, l: (kb[s], kg[s])),
        ],
        out_specs=pl.BlockSpec((tq, gw), lambda s, qt, kg, kb, f, l: (qt[s], kg[s])),
        scratch_shapes=[
            pltpu.VMEM((Q_PER_KV, tq, LANES), F32),
            pltpu.VMEM((Q_PER_KV, tq, LANES), F32),
            pltpu.VMEM((Q_PER_KV, tq, HEAD_DIM), F32),
            pltpu.VMEM((2, tq, tkc), F32),
            pltpu.VMEM((2, tq, tkc), BF16),
            pltpu.VMEM((2, tq, LANES), F32),
        ],
    )
    return pl.pallas_call(
        functools.partial(_attn_kernel, tkc=tkc, exp2_scale=HEAD_DIM ** -0.5 * LOG2_E),
        grid_spec=grid_spec,
        out_shape=jax.ShapeDtypeStruct((n, attn_w), BF16),
        compiler_params=pltpu.CompilerParams(
            dimension_semantics=("arbitrary",), vmem_limit_bytes=VMEM_LIMIT_BYTES),
        name="attention",
    )(*tabs, q, kt, v)


def _mix_out_kernel(*refs, n_x, n_a_tiles, tps, seq_len):
    x_refs = refs[:n_x]
    (a_ref, u_ref, up_ref, un_ref, pw_ref, ps_ref, wo_ref, g2_ref, rw_ref,
     x1_ref, cls_ref, ext_ref) = refs[n_x:]
    i = pl.program_id(0)
    tm = u_ref.shape[0]
    is_a = i < n_a_tiles
    tile_in_seq = jnp.where(is_a, i % tps[0], (i - n_a_tiles) % tps[1])
    t_len = jnp.where(is_a, seq_len[0], seq_len[1])
    t0 = tile_in_seq * tm

    ext_ref[0:HALO, :] = jnp.where(t0 > 0, up_ref[...], 0.0)
    ext_ref[HALO:HALO + tm, :] = u_ref[...]
    ext_ref[HALO + tm:, :] = jnp.where(t0 + tm < t_len, un_ref[...], 0.0)

    t = t0 + lax.broadcasted_iota(jnp.int32, (tm, 1), 0)
    gw = u_ref.shape[1] // len(POOL_WINDOWS)
    pooled = []
    for gi, w in enumerate(POOL_WINDOWS):
        sl = slice(gi * gw, (gi + 1) * gw)
        tot = ext_ref[HALO - w // 2:HALO - w // 2 + tm, sl]
        for o in range(-w // 2 + 1, w - w // 2):
            tot = tot + ext_ref[HALO + o:HALO + o + tm, sl]
        lo = jnp.clip(t - w // 2, 0, t_len)
        hi = jnp.clip(t + w - w // 2, 0, t_len)
        mean = tot / (hi - lo).astype(F32)
        dgrp = (mean - u_ref[:, sl]).astype(BF16)
        y = jnp.dot(dgrp, pw_ref[gi], preferred_element_type=F32)
        pooled.append((y * ps_ref[:, sl]).astype(BF16))
    mix = jnp.concatenate([a_ref[...]] + pooled, axis=-1)
    x = _pick_rows(i, n_a_tiles, x_refs)
    x1 = x + jnp.dot(mix, wo_ref[...], preferred_element_type=F32)
    x1_ref[...] = x1

    h2 = _rms(x1, g2_ref[...]).astype(BF16)
    logits = jnp.dot(h2, rw_ref[...], preferred_element_type=F32)
    lane = lax.broadcasted_iota(jnp.int32, logits.shape, 1)
    neg = jnp.float32(-jnp.inf)

    def first_max(vals):
        mx = jnp.max(vals, axis=-1, keepdims=True)
        idx = jnp.min(jnp.where(vals == mx, lane, ROUTER_LANES), axis=-1, keepdims=True)
        return idx

    g_idx = first_max(jnp.where(lane < N_EXPERT_GROUPS, logits, neg))
    e_base = N_EXPERT_GROUPS + g_idx * EXPERTS_PER_GROUP
    e_log = jnp.where((lane >= e_base) & (lane < e_base + EXPERTS_PER_GROUP), logits, neg)
    i1 = first_max(e_log)
    i2 = first_max(jnp.where(lane == i1, neg, e_log))
    e_lo = jnp.minimum(i1, i2) - e_base
    e_hi = jnp.maximum(i1, i2) - e_base
    pair = ((e_lo * (7 - e_lo)) >> 1) + e_hi - e_lo - 1
    cls_rows = jnp.broadcast_to((g_idx * PAIRS_PER_GROUP + pair).astype(F32), (tm, LANES))
    cls_ref[0] = cls_rows.T[0:1, :]


def _mix_out(xs, seqs, a, u, pool_w, pool_scale, w_out, g2, router_w):
    n, pw = u.shape
    d = w_out.shape[1]
    tm = _tiles(min(t for _, t in seqs))
    n_tiles = n // tm
    n_a_tiles = seqs[0][0] * seqs[0][1] // tm
    tps = tuple(t // tm for _, t in seqs)
    hb = tm // HALO
    n_hb = n // HALO
    if len(xs) == 2:
        x_specs = [pl.BlockSpec((tm, d), lambda i: (jnp.minimum(i, n_a_tiles - 1), 0)),
                   pl.BlockSpec((tm, d), lambda i: (jnp.maximum(i - n_a_tiles, 0), 0))]
    else:
        x_specs = [pl.BlockSpec((tm, d), lambda i: (i, 0))]
    const = lambda i: (0, 0)
    return pl.pallas_call(
        functools.partial(_mix_out_kernel, n_x=len(xs), n_a_tiles=n_a_tiles, tps=tps,
                          seq_len=tuple(t for _, t in seqs)),
        grid=(n_tiles,),
        in_specs=x_specs + [
            pl.BlockSpec((tm, a.shape[1]), lambda i: (i, 0)),
            pl.BlockSpec((tm, pw), lambda i: (i, 0)),
            pl.BlockSpec((HALO, pw), lambda i: (jnp.maximum(i * hb - 1, 0), 0)),
            pl.BlockSpec((HALO, pw), lambda i: (jnp.minimum((i + 1) * hb, n_hb - 1), 0)),
            pl.BlockSpec(pool_w.shape, lambda i: (0, 0, 0), pipeline_mode=pl.Buffered(1)),
            pl.BlockSpec((1, pw), const),
            pl.BlockSpec(w_out.shape, const, pipeline_mode=pl.Buffered(1)),
            pl.BlockSpec((1, d), const),
            pl.BlockSpec(router_w.shape, const, pipeline_mode=pl.Buffered(1)),
        ],
        out_specs=[
            pl.BlockSpec((tm, d), lambda i: (i, 0)),
            pl.BlockSpec((1, 1, tm), lambda i: (i, 0, 0)),
        ],
        out_shape=[
            jax.ShapeDtypeStruct((n, d), F32),
            jax.ShapeDtypeStruct((n_tiles, 1, tm), F32),
        ],
        scratch_shapes=[pltpu.VMEM((tm + 2 * HALO, pw), F32)],
        compiler_params=pltpu.CompilerParams(
            dimension_semantics=("arbitrary",), vmem_limit_bytes=VMEM_LIMIT_BYTES),
        name="mix_out",
    )(*xs, a, u, u, u, pool_w, pool_scale, w_out, g2, router_w)


def _moe_kernel(src_ref, dst_ref, nused_ref, grp_ref, elo_ref, ehi_ref,
                x1_hbm, g2_ref, rw_ref, wg_lo, wu_lo, wd_lo, wg_hi, wu_hi, wd_hi,
                x2_hbm, gbuf0, gbuf1, obuf0, obuf1, gsem, ssem):
    t = pl.program_id(0)
    n_used = nused_ref[0]
    tm = obuf0.shape[0]
    gbufs, obufs = (gbuf0, gbuf1), (obuf0, obuf1)

    def gather_row(tile, r, par):
        return pltpu.make_async_copy(x1_hbm.at[pl.ds(src_ref[tile * tm + r], 1)],
                                     gbufs[par].at[pl.ds(r, 1)], gsem.at[par])

    def scatter_row(tile, r, par):
        return pltpu.make_async_copy(obufs[par].at[pl.ds(r, 1)],
                                     x2_hbm.at[pl.ds(dst_ref[(tile + 1) * tm + r], 1)], ssem)

    def gather_wait(par):
        pltpu.make_async_copy(x1_hbm.at[pl.ds(0, tm)], gbufs[par], gsem.at[par]).wait()

    def scatter_wait(par):
        pltpu.make_async_copy(obufs[par], x2_hbm.at[pl.ds(0, tm)], ssem).wait()

    @pl.when(t == 0)
    def _():
        obuf1[...] = jnp.zeros(obuf1.shape, F32)

        def body(r, c):
            gather_row(0, r, 0).start()
            return c
        lax.fori_loop(0, tm, body, 0)

    def step(par):
        gather_wait(par)
        for r in range(tm):
            gather_row(t + 1, r, 1 - par).start()
        for r in range(tm):
            scatter_row(t - 1, r, 1 - par).start()
        x = gbufs[par][...]
        h2 = _rms(x, g2_ref[...]).astype(BF16)
        logits = jnp.dot(h2, rw_ref[...], preferred_element_type=F32)
        lane = lax.broadcasted_iota(jnp.int32, logits.shape, 1)

        def pick(idx):
            return jnp.sum(jnp.where(lane == idx, logits, 0.0), axis=-1, keepdims=True)

        is_grp = lane < N_EXPERT_GROUPS
        g_max = jnp.max(jnp.where(is_grp, logits, -jnp.inf), axis=-1, keepdims=True)
        g_den = jnp.sum(jnp.where(is_grp, jnp.exp(logits - g_max), 0.0), axis=-1, keepdims=True)
        grp = grp_ref[t]
        g_w = jnp.exp(pick(grp) - g_max) / g_den
        v_lo = pick(N_EXPERT_GROUPS + elo_ref[t])
        v_hi = pick(N_EXPERT_GROUPS + ehi_ref[t])
        v_max = jnp.maximum(v_lo, v_hi)
        p_lo = jnp.exp(v_lo - v_max)
        p_hi = jnp.exp(v_hi - v_max)
        w_lo = p_lo / (p_lo + p_hi) * g_w
        w_hi = p_hi / (p_lo + p_hi) * g_w

        def expert(wg, wu, wd, gate):
            g = jnp.dot(h2, wg[...], preferred_element_type=F32)
            u = jnp.dot(h2, wu[...], preferred_element_type=F32)
            hh = (g * (1.0 / (1.0 + jnp.exp(-g)))) * u * gate
            return jnp.dot(hh.astype(BF16), wd[...], preferred_element_type=F32)

        y = expert(wg_lo, wu_lo, wd_lo, w_lo) + expert(wg_hi, wu_hi, wd_hi, w_hi)
        obufs[par][...] = x + y
        scatter_wait(1 - par)

    for par in range(2):
        pl.when((t < n_used) & (t % 2 == par))(functools.partial(step, par))

    def drain(par):
        gather_wait(par)

        def body(r, c):
            scatter_row(t - 1, r, 1 - par).start()
            return c
        lax.fori_loop(0, tm, body, 0)
        scatter_wait(1 - par)

    for par in range(2):
        pl.when((t == n_used) & (t % 2 == par))(functools.partial(drain, par))


def _moe(x1, cls, g2, router_w, w_gate, w_up, w_down, tm):
    n, d = x1.shape
    n_exp, _, d_e = w_gate.shape
    n_tiles = n // tm + N_CLASSES

    classes = jnp.arange(N_CLASSES, dtype=jnp.int32)
    onehot = (cls[:, None] == classes[None, :]).astype(jnp.int32)
    csum = jnp.cumsum(onehot, axis=0)
    cnt = csum[-1]
    tiles_c = (cnt + tm - 1) // tm
    tile_end = jnp.cumsum(tiles_c)
    tile_start = tile_end - tiles_c
    n_used = tile_end[-1]
    pos = jnp.sum(onehot * (csum - 1 + tile_start[None, :] * tm), axis=1)
    spare = n + jnp.arange(n_tiles * tm, dtype=jnp.int32) % tm
    dst = spare.at[pos].set(jnp.arange(n, dtype=jnp.int32), unique_indices=True)
    src = jnp.where(dst < n, dst, 0)
    dst = jnp.concatenate([spare[:tm], dst])
    tile_id = jnp.minimum(jnp.arange(n_tiles, dtype=jnp.int32), n_used - 1)
    tile_cls = jnp.sum((tile_end[None, :] <= tile_id[:, None]).astype(jnp.int32), axis=1)
    tile_cls = jnp.minimum(tile_cls, N_CLASSES - 1)

    grp = tile_cls // PAIRS_PER_GROUP
    pair = tile_cls % PAIRS_PER_GROUP
    pair_lo = jnp.asarray(np.asarray([0, 0, 0, 1, 1, 2], np.int32))
    pair_hi = jnp.asarray(np.asarray([1, 2, 3, 2, 3, 3], np.int32))
    e_lo = grp * EXPERTS_PER_GROUP + pair_lo[pair]
    e_hi = grp * EXPERTS_PER_GROUP + pair_hi[pair]

    const = lambda t, *_: (0, 0)
    w_in_spec = lambda sel: pl.BlockSpec(
        (None, d, d_e), lambda t, src, dst, nu, grp, elo, ehi: ((elo, ehi)[sel][t], 0, 0))
    w_dn_spec = lambda sel: pl.BlockSpec(
        (None, d_e, d), lambda t, src, dst, nu, grp, elo, ehi: ((elo, ehi)[sel][t], 0, 0))
    grid_spec = pltpu.PrefetchScalarGridSpec(
        num_scalar_prefetch=6,
        grid=(n_tiles,),
        in_specs=[
            pl.BlockSpec(memory_space=pl.ANY),
            pl.BlockSpec((1, d), const),
            pl.BlockSpec(router_w.shape, const),
            w_in_spec(0), w_in_spec(0), w_dn_spec(0),
            w_in_spec(1), w_in_spec(1), w_dn_spec(1),
        ],
        out_specs=pl.BlockSpec(memory_space=pl.ANY),
        scratch_shapes=[
            pltpu.VMEM((tm, d), F32),
            pltpu.VMEM((tm, d), F32),
            pltpu.VMEM((tm, d), F32),
            pltpu.VMEM((tm, d), F32),
            pltpu.SemaphoreType.DMA((2,)),
            pltpu.SemaphoreType.DMA,
        ],
    )
    return pl.pallas_call(
        _moe_kernel,
        grid_spec=grid_spec,
        out_shape=jax.ShapeDtypeStruct((n + tm, d), F32),
        compiler_params=pltpu.CompilerParams(
            dimension_semantics=("arbitrary",), vmem_limit_bytes=VMEM_LIMIT_BYTES),
        name="moe",
    )(src, dst, n_used[None].astype(jnp.int32), grp.astype(jnp.int32), e_lo.astype(jnp.int32),
      e_hi.astype(jnp.int32), x1, g2, router_w, w_gate, w_up, w_down, w_gate, w_up, w_down)


def _final_kernel(x_ref, g_ref, o_ref):
    o_ref[...] = _rms(x_ref[...], g_ref[...])


def _final_norm(x2, g, row0, rows):
    d = x2.shape[1]
    tm = _tiles(rows)
    return pl.pallas_call(
        _final_kernel,
        grid=(rows // tm,),
        in_specs=[pl.BlockSpec((tm, d), lambda i: (i + row0 // tm, 0)),
                  pl.BlockSpec((1, d), lambda i: (0, 0))],
        out_specs=pl.BlockSpec((tm, d), lambda i: (i, 0)),
        out_shape=jax.ShapeDtypeStruct((rows, d), F32),
        compiler_params=pltpu.CompilerParams(
            dimension_semantics=("arbitrary",), vmem_limit_bytes=VMEM_LIMIT_BYTES),
        name="final_norm",
    )(x2, g)


def _rope_tables(t_max):
    rows = t_max // GRID_W
    row = jnp.repeat(jnp.arange(rows, dtype=F32), GRID_W)
    col = jnp.tile(jnp.arange(GRID_W, dtype=F32), rows)
    axis_dim = HEAD_DIM // 2
    freqs = ROPE_THETA ** (-jnp.arange(0, axis_dim, 2, dtype=F32) / axis_dim)
    ar, ac = row[:, None] * freqs, col[:, None] * freqs
    cos_t = jnp.concatenate([jnp.cos(ar), jnp.cos(ar), jnp.cos(ac), jnp.cos(ac)], axis=-1)
    sin_t = jnp.concatenate([-jnp.sin(ar), jnp.sin(ar), -jnp.sin(ac), jnp.sin(ac)], axis=-1)
    return cos_t, sin_t


def kernel(x_prompt, x_sample, norm1_g, w_in, q_norm_g, k_norm_g, pool_w, pool_scale, w_out,
           norm2_g, router_group, router_expert, w_gate, w_up, w_down, final_g):
    depth, d, _ = w_in.shape
    seqs = (x_prompt.shape[:2], x_sample.shape[:2])
    n_a = seqs[0][0] * seqs[0][1]
    n_b = seqs[1][0] * seqs[1][1]
    cos_t, sin_t = _rope_tables(max(t for _, t in seqs))
    router_w = jnp.concatenate([router_group, router_expert], axis=-1)
    router_w = jnp.pad(router_w, ((0, 0), (0, 0), (0, ROUTER_LANES - router_w.shape[-1])))
    router_w = router_w.astype(BF16)
    w_in_b, w_out_b, pool_w_b = w_in.astype(BF16), w_out.astype(BF16), pool_w.astype(BF16)
    w_gate_b, w_up_b, w_down_b = w_gate.astype(BF16), w_up.astype(BF16), w_down.astype(BF16)
    moe_tm = min(256, n_a, n_b)

    xs = (x_prompt.reshape(n_a, d), x_sample.reshape(n_b, d))
    for l in range(depth):
        q, kt, v, u = _in_proj(xs, seqs, norm1_g[l][None], w_in_b[l], q_norm_g[l][None],
                               k_norm_g[l][None], cos_t, sin_t)
        a = _attention(q, kt, v, seqs)
        x1, cls = _mix_out(xs, seqs, a, u, pool_w_b[l], pool_scale[l][None], w_out_b[l],
                           norm2_g[l][None], router_w[l])
        x2 = _moe(x1, cls.reshape(-1).astype(jnp.int32), norm2_g[l][None], router_w[l], w_gate_b[l], w_up_b[l],
                  w_down_b[l], moe_tm)
        xs = (x2,)
    y_a = _final_norm(xs[0], final_g[None], 0, n_a).reshape(x_prompt.shape)
    y_b = _final_norm(xs[0], final_g[None], n_a, n_b).reshape(x_sample.shape)
    return (y_a, y_b)
```

```python
import functools

import numpy as np
import jax
import jax.numpy as jnp
from jax import lax
from jax.experimental import pallas as pl
from jax.experimental.pallas import tpu as pltpu

F32 = jnp.float32
BF16 = jnp.bfloat16

GRID_W = 64
HEAD_DIM = 128
N_KV_HEADS = 2
Q_PER_KV = 4
POOL_WINDOWS = (2, 4, 8, 16)
ROPE_THETA = 10000.0
N_EXPERT_GROUPS = 4
EXPERTS_PER_GROUP = 4
PAIRS_PER_GROUP = 6
N_CLASSES = N_EXPERT_GROUPS * PAIRS_PER_GROUP
EPS = 1e-6

LANES = 128
SUBLANES = 8
VMEM_LIMIT_BYTES = 56 * 1024 * 1024

HALO = SUBLANES
ROUTER_LANES = LANES
SOFTMAX_ROWS = 32
LOG2_E = 1.4426950408889634
Q_EXP2_SCALE = HEAD_DIM ** -0.5 * LOG2_E


def _tiles(n_rows):
    tm = min(512, n_rows)
    return tm


def _rms(x, g):
    ms = jnp.mean(x * x, axis=-1, keepdims=True)
    return x * lax.rsqrt(ms + EPS) * g


def _pick_rows(i, n_a_tiles, refs):
    if len(refs) == 2:
        return jnp.where(i < n_a_tiles, refs[0][...], refs[1][...])
    return refs[0][...]


def _in_proj_kernel(*refs, n_x, n_a_tiles, attn_w, kv_w):
    x_refs = refs[:n_x]
    (g1_ref, w_ref, qg_ref, kg_ref, cos_ref, sin_ref,
     q_ref, kt_ref, v_ref, u_ref) = refs[n_x:]
    i = pl.program_id(0)
    x = _pick_rows(i, n_a_tiles, x_refs)
    h = _rms(x, g1_ref[...]).astype(BF16)
    proj = jnp.dot(h, w_ref[...], preferred_element_type=F32)

    cos = cos_ref[...]
    sin = sin_ref[...]
    lane = lax.broadcasted_iota(jnp.int32, cos.shape, 1)
    low_half = (lane & (HEAD_DIM // 4)) == 0

    def norm_rope(xh, g):
        y = _rms(xh, g)
        partner = jnp.where(low_half, pltpu.roll(y, HEAD_DIM - HEAD_DIM // 4, 1),
                            pltpu.roll(y, HEAD_DIM // 4, 1))
        return y * cos + partner * sin

    qg = qg_ref[...]
    for hq in range(attn_w // HEAD_DIM):
        sl = slice(hq * HEAD_DIM, (hq + 1) * HEAD_DIM)
        q_ref[:, sl] = (norm_rope(proj[:, sl], qg) * Q_EXP2_SCALE).astype(BF16)
    kg = kg_ref[...]
    for hk in range(N_KV_HEADS):
        sl = slice(attn_w + hk * HEAD_DIM, attn_w + (hk + 1) * HEAD_DIM)
        kt_ref[hk * HEAD_DIM:(hk + 1) * HEAD_DIM, :] = norm_rope(proj[:, sl], kg).T.astype(BF16)
    v_ref[...] = proj[:, attn_w + kv_w:attn_w + 2 * kv_w].astype(BF16)
    u_ref[...] = proj[:, attn_w + 2 * kv_w:]


def _in_proj(xs, seqs, g1, w_in, qg, kg, cos_t, sin_t, layer):
    n = sum(b * t for b, t in seqs)
    _, d, in_w = w_in.shape
    attn_w = d // 2
    kv_w = N_KV_HEADS * HEAD_DIM
    pool_w = in_w - attn_w - 2 * kv_w
    tm = _tiles(min(t for _, t in seqs))
    n_tiles = n // tm
    n_a_tiles = seqs[0][0] * seqs[0][1] // tm
    tps = [t // tm for _, t in seqs]

    def pos_map(i):
        return (jnp.where(i < n_a_tiles, i % tps[0], (i - n_a_tiles) % tps[1]), 0)

    if len(xs) == 2:
        x_specs = [pl.BlockSpec((tm, d), lambda i: (jnp.minimum(i, n_a_tiles - 1), 0)),
                   pl.BlockSpec((tm, d), lambda i: (jnp.maximum(i - n_a_tiles, 0), 0))]
    else:
        x_specs = [pl.BlockSpec((tm, d), lambda i: (i, 0))]
    const = lambda i: (0, 0)
    return pl.pallas_call(
        functools.partial(_in_proj_kernel, n_x=len(xs), n_a_tiles=n_a_tiles,
                          attn_w=attn_w, kv_w=kv_w),
        grid=(n_tiles,),
        in_specs=x_specs + [
            pl.BlockSpec((1, d), const),
            pl.BlockSpec((None, d, in_w), lambda i: (layer, 0, 0), pipeline_mode=pl.Buffered(1)),
            pl.BlockSpec((1, HEAD_DIM), const),
            pl.BlockSpec((1, HEAD_DIM), const),
            pl.BlockSpec((tm, HEAD_DIM), pos_map),
            pl.BlockSpec((tm, HEAD_DIM), pos_map),
        ],
        out_specs=[
            pl.BlockSpec((tm, attn_w), lambda i: (i, 0)),
            pl.BlockSpec((kv_w, tm), lambda i: (0, i)),
            pl.BlockSpec((tm, kv_w), lambda i: (i, 0)),
            pl.BlockSpec((tm, pool_w), lambda i: (i, 0)),
        ],
        out_shape=[
            jax.ShapeDtypeStruct((n, attn_w), BF16),
            jax.ShapeDtypeStruct((kv_w, n), BF16),
            jax.ShapeDtypeStruct((n, kv_w), BF16),
            jax.ShapeDtypeStruct((n, pool_w), F32),
        ],
        compiler_params=pltpu.CompilerParams(
            dimension_semantics=("arbitrary",), vmem_limit_bytes=VMEM_LIMIT_BYTES),
        name="in_proj",
    )(*xs, g1, w_in, qg, kg, cos_t, sin_t)


def _attn_kernel(qt_ref, kg_ref, kb_ref, first_ref, last_ref,
                 q_ref, kt_ref, v_ref, o_ref, m_ref, l_ref, acc_ref, s_ref, p_ref, al_ref,
                 *, tkc):
    s_idx = pl.program_id(0)
    tq = q_ref.shape[0]
    n_chunks = kt_ref.shape[1] // tkc
    tqs = s_ref.shape[1]
    iters = [(qs, c, hq) for qs in range(tq // tqs) for c in range(n_chunks)
             for hq in range(Q_PER_KV)]

    @pl.when(first_ref[s_idx] == 1)
    def _():
        m_ref[...] = jnp.full(m_ref.shape, -jnp.inf, F32)
        l_ref[...] = jnp.zeros(l_ref.shape, F32)
        acc_ref[...] = jnp.zeros(acc_ref.shape, F32)

    def scores(k):
        qs, c, hq = iters[k]
        s_ref[k % 2] = jnp.dot(q_ref[qs * tqs:(qs + 1) * tqs, hq * HEAD_DIM:(hq + 1) * HEAD_DIM],
                               kt_ref[:, c * tkc:(c + 1) * tkc], preferred_element_type=F32)

    def softmax(k):
        qs, _, hq = iters[k]
        slot = k % 2
        for r0 in range(0, tqs, SOFTMAX_ROWS):
            rows = slice(r0, r0 + SOFTMAX_ROWS)
            qrows = slice(qs * tqs + r0, qs * tqs + r0 + SOFTMAX_ROWS)
            m_prev = m_ref[hq, qrows, :]
            m_new = jnp.maximum(m_prev, jnp.max(s_ref[slot, rows, :], axis=-1, keepdims=True))
            al_ref[slot, rows, :] = jnp.exp2(m_prev - m_new)
            m_ref[hq, qrows, :] = m_new
        for r0 in range(0, tqs, SOFTMAX_ROWS):
            rows = slice(r0, r0 + SOFTMAX_ROWS)
            qrows = slice(qs * tqs + r0, qs * tqs + r0 + SOFTMAX_ROWS)
            m_new = m_ref[hq, qrows, :]
            p_sum = al_ref[slot, rows, :] * l_ref[hq, qrows, :]
            for j in range(tkc // LANES):
                cols = slice(j * LANES, (j + 1) * LANES)
                p = jnp.exp2(s_ref[slot, rows, cols] - m_new)
                p_ref[slot, rows, cols] = p.astype(BF16)
                p_sum = p_sum + p
            l_ref[hq, qrows, :] = p_sum

    def weighted_values(k):
        qs, c, hq = iters[k]
        qrows = slice(qs * tqs, (qs + 1) * tqs)
        acc_ref[hq, qrows, :] = acc_ref[hq, qrows, :] * al_ref[k % 2] + jnp.dot(
            p_ref[k % 2], v_ref[c * tkc:(c + 1) * tkc, :], preferred_element_type=F32)

    scores(0)
    for k in range(len(iters)):
        if k + 1 < len(iters):
            scores(k + 1)
        if k >= 1:
            weighted_values(k - 1)
        softmax(k)
    weighted_values(len(iters) - 1)

    @pl.when(last_ref[s_idx] == 1)
    def _():
        for hq in range(Q_PER_KV):
            l_row = jnp.sum(l_ref[hq], axis=-1, keepdims=True)
            o_ref[:, hq * HEAD_DIM:(hq + 1) * HEAD_DIM] = (acc_ref[hq] / l_row).astype(BF16)


def _attention(q, kt, v, seqs):
    n, attn_w = q.shape
    t_min = min(t for _, t in seqs)
    tq = min(1024, t_min)
    tqs = min(512, tq)
    tkb = min(2048, t_min)
    tkc = min(512, tkb)
    qt, kg, kb, first, last = [], [], [], [], []
    row0 = 0
    for b, t in seqs:
        for _ in range(b):
            for g in range(N_KV_HEADS):
                for iq in range(t // tq):
                    for ik in range(t // tkb):
                        qt.append(row0 // tq + iq)
                        kg.append(g)
                        kb.append(row0 // tkb + ik)
                        first.append(int(ik == 0))
                        last.append(int(ik == t // tkb - 1))
            row0 += t
    tabs = [jnp.asarray(np.asarray(a, np.int32)) for a in (qt, kg, kb, first, last)]
    gw = Q_PER_KV * HEAD_DIM
    grid_spec = pltpu.PrefetchScalarGridSpec(
        num_scalar_prefetch=5,
        grid=(len(qt),),
        in_specs=[
            pl.BlockSpec((tq, gw), lambda s, qt, kg, kb, f, l: (qt[s], kg[s])),
            pl.BlockSpec((HEAD_DIM, tkb), lambda s, qt, kg, kb, f, l: (kg[s], kb[s])),
            pl.BlockSpec((tkb, HEAD_DIM), lambda s, qt, kg, kb, f, l: (kb[s], kg[s])),
        ],
        out_specs=pl.BlockSpec((tq, gw), lambda s, qt, kg, kb, f, l: (qt[s], kg[s])),
        scratch_shapes=[
            pltpu.VMEM((Q_PER_KV, tq, LANES), F32),
            pltpu.VMEM((Q_PER_KV, tq, LANES), F32),
            pltpu.VMEM((Q_PER_KV, tq, HEAD_DIM), F32),
            pltpu.VMEM((2, tqs, tkc), F32),
            pltpu.VMEM((2, tqs, tkc), BF16),
            pltpu.VMEM((2, tqs, LANES), F32),
        ],
    )
    return pl.pallas_call(
        functools.partial(_attn_kernel, tkc=tkc),
        grid_spec=grid_spec,
        out_shape=jax.ShapeDtypeStruct((n, attn_w), BF16),
        compiler_params=pltpu.CompilerParams(
            dimension_semantics=("arbitrary",), vmem_limit_bytes=VMEM_LIMIT_BYTES),
        name="attention",
    )(*tabs, q, kt, v)


def _mix_out_kernel(*refs, n_x, n_a_tiles, tps, seq_len):
    x_refs = refs[:n_x]
    (a_ref, u_ref, up_ref, un_ref, pw_ref, ps_ref, wo_ref, g2_ref, rw_ref,
     x1_ref, cls_ref, ext_ref) = refs[n_x:]
    i = pl.program_id(0)
    tm = u_ref.shape[0]
    is_a = i < n_a_tiles
    tile_in_seq = jnp.where(is_a, i % tps[0], (i - n_a_tiles) % tps[1])
    t_len = jnp.where(is_a, seq_len[0], seq_len[1])
    t0 = tile_in_seq * tm

    ext_ref[0:HALO, :] = jnp.where(t0 > 0, up_ref[...], 0.0)
    ext_ref[HALO:HALO + tm, :] = u_ref[...]
    ext_ref[HALO + tm:, :] = jnp.where(t0 + tm < t_len, un_ref[...], 0.0)

    gw = u_ref.shape[1] // len(POOL_WINDOWS)
    hm = tm // 2
    neg = jnp.float32(-jnp.inf)

    def pool_mix(h):
        r0 = h * hm
        t = t0 + r0 + lax.broadcasted_iota(jnp.int32, (hm, 1), 0)
        pooled = []
        for gi, w in enumerate(POOL_WINDOWS):
            sl = slice(gi * gw, (gi + 1) * gw)
            run = ext_ref[r0:r0 + hm + 2 * HALO, sl]
            n_ext = run.shape[0]
            span = 1
            while span < w // 2:
                run = run + pltpu.roll(run, n_ext - span, 0)
                span *= 2
            tot = (run + pltpu.roll(run, w // 2, 0))[HALO:HALO + hm]
            lo = jnp.clip(t - w // 2, 0, t_len)
            hi = jnp.clip(t + w - w // 2, 0, t_len)
            mean = tot / (hi - lo).astype(F32)
            dgrp = (mean - u_ref[r0:r0 + hm, sl]).astype(BF16)
            y = jnp.dot(dgrp, pw_ref[gi], preferred_element_type=F32)
            pooled.append((y * ps_ref[:, sl]).astype(BF16))
        return jnp.concatenate([a_ref[r0:r0 + hm, :]] + pooled, axis=-1)

    def project(h, mix):
        rows = slice(h * hm, (h + 1) * hm)
        if len(x_refs) == 2:
            x = jnp.where(i < n_a_tiles, x_refs[0][rows, :], x_refs[1][rows, :])
        else:
            x = x_refs[0][rows, :]
        x1 = x + jnp.dot(mix, wo_ref[...], preferred_element_type=F32)
        x1_ref[rows, :] = x1
        return x1

    def route(h, x1):
        h2 = _rms(x1, g2_ref[...]).astype(BF16)
        logits = jnp.dot(h2, rw_ref[...], preferred_element_type=F32)
        lane = lax.broadcasted_iota(jnp.int32, logits.shape, 1)

        def first_max(vals):
            mx = jnp.max(vals, axis=-1, keepdims=True)
            return jnp.min(jnp.where(vals == mx, lane, ROUTER_LANES), axis=-1, keepdims=True)

        g_idx = first_max(jnp.where(lane < N_EXPERT_GROUPS, logits, neg))
        e_base = N_EXPERT_GROUPS + g_idx * EXPERTS_PER_GROUP
        e_log = jnp.where((lane >= e_base) & (lane < e_base + EXPERTS_PER_GROUP), logits, neg)
        i1 = first_max(e_log)
        i2 = first_max(jnp.where(lane == i1, neg, e_log))
        e_lo = jnp.minimum(i1, i2) - e_base
        e_hi = jnp.maximum(i1, i2) - e_base
        pair = ((e_lo * (7 - e_lo)) >> 1) + e_hi - e_lo - 1
        cls_rows = jnp.broadcast_to((g_idx * PAIRS_PER_GROUP + pair).astype(F32), (hm, LANES))
        cls_ref[0, :, h * hm:(h + 1) * hm] = cls_rows.T[0:1, :]

    x1_0 = project(0, pool_mix(0))
    mix_1 = pool_mix(1)
    x1_1 = project(1, mix_1)
    route(0, x1_0)
    route(1, x1_1)


def _mix_out(xs, seqs, a, u, pool_w, pool_scale, w_out, g2, router_w, layer):
    n, pw = u.shape
    d = w_out.shape[-1]
    layer_block = lambda w: pl.BlockSpec((None,) + w.shape[1:],
                                         lambda i: (layer,) + (0,) * (w.ndim - 1),
                                         pipeline_mode=pl.Buffered(1))
    tm = _tiles(min(t for _, t in seqs))
    n_tiles = n // tm
    n_a_tiles = seqs[0][0] * seqs[0][1] // tm
    tps = tuple(t // tm for _, t in seqs)
    hb = tm // HALO
    n_hb = n // HALO
    if len(xs) == 2:
        x_specs = [pl.BlockSpec((tm, d), lambda i: (jnp.minimum(i, n_a_tiles - 1), 0)),
                   pl.BlockSpec((tm, d), lambda i: (jnp.maximum(i - n_a_tiles, 0), 0))]
    else:
        x_specs = [pl.BlockSpec((tm, d), lambda i: (i, 0))]
    const = lambda i: (0, 0)
    return pl.pallas_call(
        functools.partial(_mix_out_kernel, n_x=len(xs), n_a_tiles=n_a_tiles, tps=tps,
                          seq_len=tuple(t for _, t in seqs)),
        grid=(n_tiles,),
        in_specs=x_specs + [
            pl.BlockSpec((tm, a.shape[1]), lambda i: (i, 0)),
            pl.BlockSpec((tm, pw), lambda i: (i, 0)),
            pl.BlockSpec((HALO, pw), lambda i: (jnp.maximum(i * hb - 1, 0), 0)),
            pl.BlockSpec((HALO, pw), lambda i: (jnp.minimum((i + 1) * hb, n_hb - 1), 0)),
            layer_block(pool_w),
            pl.BlockSpec((1, pw), const),
            layer_block(w_out),
            pl.BlockSpec((1, d), const),
            layer_block(router_w),
        ],
        out_specs=[
            pl.BlockSpec((tm, d), lambda i: (i, 0)),
            pl.BlockSpec((1, 1, tm), lambda i: (i, 0, 0)),
        ],
        out_shape=[
            jax.ShapeDtypeStruct((n, d), F32),
            jax.ShapeDtypeStruct((n_tiles, 1, tm), F32),
        ],
        scratch_shapes=[pltpu.VMEM((tm + 2 * HALO, pw), F32)],
        compiler_params=pltpu.CompilerParams(
            dimension_semantics=("arbitrary",), vmem_limit_bytes=VMEM_LIMIT_BYTES),
        name="mix_out",
    )(*xs, a, u, u, u, pool_w, pool_scale, w_out, g2, router_w)


def _moe_kernel(src_ref, dst_ref, nused_ref, grp_ref, elo_ref, ehi_ref,
                x1_hbm, g2_ref, rw_ref, wg_lo, wu_lo, wd_lo, wg_hi, wu_hi, wd_hi,
                x2_hbm, gbuf0, gbuf1, obuf0, obuf1, h2_ref, ylo_ref, gate_ref, gsem, ssem):
    t = pl.program_id(0)
    n_used = nused_ref[0]
    tm = obuf0.shape[0]
    gbufs, obufs = (gbuf0, gbuf1), (obuf0, obuf1)

    def gather_row(tile, r, par):
        return pltpu.make_async_copy(x1_hbm.at[pl.ds(src_ref[tile * tm + r], 1)],
                                     gbufs[par].at[pl.ds(r, 1)], gsem.at[par])

    def scatter_row(tile, r, par):
        return pltpu.make_async_copy(obufs[par].at[pl.ds(r, 1)],
                                     x2_hbm.at[pl.ds(dst_ref[(tile + 2) * tm + r], 1)], ssem)

    def gather_wait(par):
        pltpu.make_async_copy(x1_hbm.at[pl.ds(0, tm)], gbufs[par], gsem.at[par]).wait()

    def scatter_wait():
        pltpu.make_async_copy(obuf0, x2_hbm.at[pl.ds(0, tm)], ssem).wait()

    @pl.when(t == 0)
    def _():
        obuf0[...] = jnp.zeros(obuf0.shape, F32)
        obuf1[...] = jnp.zeros(obuf1.shape, F32)

        def body(r, c):
            gather_row(0, r, 0).start()
            scatter_row(-2, r, 0).start()
            return c
        lax.fori_loop(0, tm, body, 0)

    def expert(h2, wg, wu, wd, gate):
        g = jnp.dot(h2, wg[...], preferred_element_type=F32)
        u = jnp.dot(h2, wu[...], preferred_element_type=F32)
        hh = (g * (1.0 / (1.0 + jnp.exp(-g)))) * u * gate
        return jnp.dot(hh.astype(BF16), wd[...], preferred_element_type=F32)

    def first_expert(par):
        gather_wait(par)
        for r in range(tm):
            gather_row(t + 1, r, 1 - par).start()
        x = gbufs[par][...]
        h2 = _rms(x, g2_ref[...]).astype(BF16)
        h2_ref[...] = h2
        logits = jnp.dot(h2, rw_ref[...], preferred_element_type=F32)
        lane = lax.broadcasted_iota(jnp.int32, logits.shape, 1)

        def pick(idx):
            return jnp.sum(jnp.where(lane == idx, logits, 0.0), axis=-1, keepdims=True)

        is_grp = lane < N_EXPERT_GROUPS
        g_max = jnp.max(jnp.where(is_grp, logits, -jnp.inf), axis=-1, keepdims=True)
        g_den = jnp.sum(jnp.where(is_grp, jnp.exp(logits - g_max), 0.0), axis=-1, keepdims=True)
        grp = grp_ref[t]
        g_w = jnp.exp(pick(grp) - g_max) / g_den
        v_lo = pick(N_EXPERT_GROUPS + elo_ref[t])
        v_hi = pick(N_EXPERT_GROUPS + ehi_ref[t])
        v_max = jnp.maximum(v_lo, v_hi)
        p_lo = jnp.exp(v_lo - v_max)
        p_hi = jnp.exp(v_hi - v_max)
        w_lo = p_lo / (p_lo + p_hi) * g_w
        gate_ref[...] = jnp.broadcast_to(p_hi / (p_lo + p_hi) * g_w, gate_ref.shape)
        ylo_ref[...] = x + expert(h2, wg_lo, wu_lo, wd_lo, w_lo)

    def second_expert(par):
        scatter_wait()
        for r in range(tm):
            scatter_row(t - 1, r, 1 - par).start()
        y_hi = expert(h2_ref[...], wg_hi, wu_hi, wd_hi, gate_ref[:, 0:1])
        obufs[par][...] = ylo_ref[...] + y_hi

    for par in range(2):
        pl.when((t < n_used) & (t % 2 == par))(functools.partial(first_expert, par))
    for par in range(2):
        pl.when((t + 1 <= n_used) & (t % 2 == par))(functools.partial(second_expert, par))

    def drain(par):
        gather_wait(par)
        scatter_wait()

        def body(r, c):
            scatter_row(t - 1, r, 1 - par).start()
            return c
        lax.fori_loop(0, tm, body, 0)
        scatter_wait()

    for par in range(2):
        pl.when((t == n_used) & (t % 2 == par))(functools.partial(drain, par))


def _moe(x1, cls, g2, router_w, w_gate, w_up, w_down, layer, tm):
    n, d = x1.shape
    d_e = w_gate.shape[-1]
    n_tiles = n // tm + N_CLASSES

    classes = jnp.arange(N_CLASSES, dtype=jnp.int32)
    onehot = (cls[:, None] == classes[None, :]).astype(jnp.int32)
    csum = jnp.cumsum(onehot, axis=0)
    cnt = csum[-1]
    tiles_c = (cnt + tm - 1) // tm
    tile_end = jnp.cumsum(tiles_c)
    tile_start = tile_end - tiles_c
    n_used = tile_end[-1]
    pos = jnp.sum(onehot * (csum - 1 + tile_start[None, :] * tm), axis=1)
    spare = n + jnp.arange(n_tiles * tm, dtype=jnp.int32) % tm
    dst = spare.at[pos].set(jnp.arange(n, dtype=jnp.int32), unique_indices=True)
    src = jnp.where(dst < n, dst, 0)
    dst = jnp.concatenate([spare[:tm], spare[:tm], dst])
    tile_id = jnp.minimum(jnp.arange(n_tiles, dtype=jnp.int32), n_used - 1)
    tile_cls = jnp.sum((tile_end[None, :] <= tile_id[:, None]).astype(jnp.int32), axis=1)
    tile_cls = jnp.minimum(tile_cls, N_CLASSES - 1)

    grp = tile_cls // PAIRS_PER_GROUP
    pair = tile_cls % PAIRS_PER_GROUP
    pair_lo = jnp.asarray(np.asarray([0, 0, 0, 1, 1, 2], np.int32))
    pair_hi = jnp.asarray(np.asarray([1, 2, 3, 2, 3, 3], np.int32))
    e_lo = grp * EXPERTS_PER_GROUP + pair_lo[pair]
    e_hi = grp * EXPERTS_PER_GROUP + pair_hi[pair]

    const = lambda t, *_: (0, 0)
    w_in_spec = lambda sel: pl.BlockSpec(
        (None, None, d, d_e),
        lambda t, src, dst, nu, grp, elo, ehi: (layer, (elo, ehi)[sel][t], 0, 0))
    w_dn_spec = lambda sel: pl.BlockSpec(
        (None, None, d_e, d),
        lambda t, src, dst, nu, grp, elo, ehi: (layer, (elo, ehi)[sel][t], 0, 0))
    grid_spec = pltpu.PrefetchScalarGridSpec(
        num_scalar_prefetch=6,
        grid=(n_tiles,),
        in_specs=[
            pl.BlockSpec(memory_space=pl.ANY),
            pl.BlockSpec((1, d), const),
            pl.BlockSpec((None,) + router_w.shape[1:], lambda t, *_: (layer, 0, 0)),
            w_in_spec(0), w_in_spec(0), w_dn_spec(0),
            w_in_spec(1), w_in_spec(1), w_dn_spec(1),
        ],
        out_specs=pl.BlockSpec(memory_space=pl.ANY),
        scratch_shapes=[
            pltpu.VMEM((tm, d), F32),
            pltpu.VMEM((tm, d), F32),
            pltpu.VMEM((tm, d), F32),
            pltpu.VMEM((tm, d), F32),
            pltpu.VMEM((tm, d), BF16),
            pltpu.VMEM((tm, d), F32),
            pltpu.VMEM((tm, LANES), F32),
            pltpu.SemaphoreType.DMA((2,)),
            pltpu.SemaphoreType.DMA,
        ],
    )
    return pl.pallas_call(
        _moe_kernel,
        grid_spec=grid_spec,
        out_shape=jax.ShapeDtypeStruct((n + tm, d), F32),
        compiler_params=pltpu.CompilerParams(
            dimension_semantics=("arbitrary",), vmem_limit_bytes=VMEM_LIMIT_BYTES),
        name="moe",
    )(src, dst, n_used[None].astype(jnp.int32), grp.astype(jnp.int32), e_lo.astype(jnp.int32),
      e_hi.astype(jnp.int32), x1, g2, router_w, w_gate, w_up, w_down, w_gate, w_up, w_down)


def _final_kernel(x_ref, g_ref, o_ref):
    o_ref[...] = _rms(x_ref[...], g_ref[...])


def _final_norm(x2, g, row0, rows):
    d = x2.shape[1]
    tm = _tiles(rows)
    return pl.pallas_call(
        _final_kernel,
        grid=(rows // tm,),
        in_specs=[pl.BlockSpec((tm, d), lambda i: (i + row0 // tm, 0)),
                  pl.BlockSpec((1, d), lambda i: (0, 0))],
        out_specs=pl.BlockSpec((tm, d), lambda i: (i, 0)),
        out_shape=jax.ShapeDtypeStruct((rows, d), F32),
        compiler_params=pltpu.CompilerParams(
            dimension_semantics=("arbitrary",), vmem_limit_bytes=VMEM_LIMIT_BYTES),
        name="final_norm",
    )(x2, g)


def _rope_tables(t_max):
    rows = t_max // GRID_W
    row = jnp.repeat(jnp.arange(rows, dtype=F32), GRID_W)
    col = jnp.tile(jnp.arange(GRID_W, dtype=F32), rows)
    axis_dim = HEAD_DIM // 2
    freqs = ROPE_THETA ** (-jnp.arange(0, axis_dim, 2, dtype=F32) / axis_dim)
    ar, ac = row[:, None] * freqs, col[:, None] * freqs
    cos_t = jnp.concatenate([jnp.cos(ar), jnp.cos(ar), jnp.cos(ac), jnp.cos(ac)], axis=-1)
    sin_t = jnp.concatenate([-jnp.sin(ar), jnp.sin(ar), -jnp.sin(ac), jnp.sin(ac)], axis=-1)
    return cos_t, sin_t


def kernel(x_prompt, x_sample, norm1_g, w_in, q_norm_g, k_norm_g, pool_w, pool_scale, w_out,
           norm2_g, router_group, router_expert, w_gate, w_up, w_down, final_g):
    depth, d, _ = w_in.shape
    seqs = (x_prompt.shape[:2], x_sample.shape[:2])
    n_a = seqs[0][0] * seqs[0][1]
    n_b = seqs[1][0] * seqs[1][1]
    cos_t, sin_t = _rope_tables(max(t for _, t in seqs))
    router_w = jnp.concatenate([router_group, router_expert], axis=-1)
    router_w = jnp.pad(router_w, ((0, 0), (0, 0), (0, ROUTER_LANES - router_w.shape[-1])))
    router_w = router_w.astype(BF16)
    w_in_b, w_out_b, pool_w_b = w_in.astype(BF16), w_out.astype(BF16), pool_w.astype(BF16)
    w_gate_b, w_up_b, w_down_b = w_gate.astype(BF16), w_up.astype(BF16), w_down.astype(BF16)
    moe_tm = min(256, n_a, n_b)

    xs = (x_prompt.reshape(n_a, d), x_sample.reshape(n_b, d))
    for l in range(depth):
        q, kt, v, u = _in_proj(xs, seqs, norm1_g[l][None], w_in_b, q_norm_g[l][None],
                               k_norm_g[l][None], cos_t, sin_t, l)
        a = _attention(q, kt, v, seqs)
        x1, cls = _mix_out(xs, seqs, a, u, pool_w_b, pool_scale[l][None], w_out_b,
                           norm2_g[l][None], router_w, l)
        x2 = _moe(x1, cls.reshape(-1).astype(jnp.int32), norm2_g[l][None], router_w,
                  w_gate_b, w_up_b, w_down_b, l, moe_tm)
        xs = (x2,)
    y_a = _final_norm(xs[0], final_g[None], 0, n_a).reshape(x_prompt.shape)
    y_b = _final_norm(xs[0], final_g[None], n_a, n_b).reshape(x_sample.shape)
    return (y_a, y_b)
```
